```python
import math
import jax, jax.numpy as jnp
from jax import lax
import numpy as np

D_MODEL = 2048
BATCH = 4
SEQ = 2048
DEPTH = 1

D_MIX = D_MODEL
ATTN_WIDTH = D_MIX // 2
GMLP_WIDTH = D_MIX - ATTN_WIDTH
ATTN_HEAD_DIM = 128
N_ATTN_HEADS = ATTN_WIDTH // ATTN_HEAD_DIM
DIFF_HALF = ATTN_HEAD_DIM // 2
ROPE_DIM = DIFF_HALF // 4
ROPE_THETA = 500000.0
N_GMLP_HEADS = 8
GMLP_HEAD_DIM = GMLP_WIDTH // N_GMLP_HEADS
CHUNK = 128
Q_BLOCK = 128
QK_WIDTH = N_ATTN_HEADS * 2 * DIFF_HALF
IN_WIDTH = 2 * QK_WIDTH + ATTN_WIDTH + 2 * GMLP_WIDTH
D_FF = 5632
CONV_WIDTH = 3
PLE_DIM = 256
EPS = 1e-6

kernel_name = "hybrid_diffattn_gmlp_convffn_encoder"


def rms_norm(x, g):
    xf = x.astype(jnp.float32)
    y = xf * lax.rsqrt(jnp.mean(xf * xf, axis=-1, keepdims=True) + EPS)
    return (y * g.astype(jnp.float32)).astype(x.dtype)


def layer_norm(x, g, b):
    xf = x.astype(jnp.float32)
    mu = jnp.mean(xf, axis=-1, keepdims=True)
    var = jnp.mean(jnp.square(xf - mu), axis=-1, keepdims=True)
    y = (xf - mu) * lax.rsqrt(var + EPS)
    return (y * g.astype(jnp.float32) + b.astype(jnp.float32)).astype(x.dtype)


def rope_tables(positions, dtype):
    inv_freq = ROPE_THETA ** (-jnp.arange(0, ROPE_DIM, 2, dtype=jnp.float32) / ROPE_DIM)
    ang = positions.astype(jnp.float32)[..., None] * inv_freq
    cos = jnp.cos(ang)[:, :, None, None, :].astype(dtype)
    sin = jnp.sin(ang)[:, :, None, None, :].astype(dtype)
    return cos, sin


def apply_partial_rope(t, cos, sin):
    half = ROPE_DIM // 2
    r1 = t[..., :half]
    r2 = t[..., half:ROPE_DIM]
    rotated = jnp.concatenate([r1 * cos - r2 * sin, r2 * cos + r1 * sin], axis=-1)
    return jnp.concatenate([rotated, t[..., ROPE_DIM:]], axis=-1)


def diff_attention(q, k, v, lam, lambda_init, g_subln):
    B, S = q.shape[0], q.shape[1]
    nb = S // Q_BLOCK
    scale = DIFF_HALF ** -0.5
    qb = q.reshape(B, nb, Q_BLOCK, N_ATTN_HEADS, 2, DIFF_HALF).transpose(1, 0, 2, 3, 4, 5)

    def block(qi):
        s = jnp.einsum('bqhcd,bkhcd->bhcqk', qi, k).astype(jnp.float32) * scale
        pr = jax.nn.softmax(s, axis=-1)
        a = pr[:, :, 0] - lam * pr[:, :, 1]
        return jnp.einsum('bhqk,bkhd->bqhd', a.astype(v.dtype), v)

    o = lax.map(block, qb)
    o = o.transpose(1, 0, 2, 3, 4).reshape(B, S, N_ATTN_HEADS, ATTN_HEAD_DIM)
    o = rms_norm(o, g_subln) * (1.0 - lambda_init)
    return o.reshape(B, S, ATTN_WIDTH)


def spatial_gating(u, v, ln_g, ln_b, w_s, b_s):
    B, S = u.shape[0], u.shape[1]
    u = jax.nn.gelu(u, approximate=False)
    v = layer_norm(jax.nn.gelu(v, approximate=False), ln_g, ln_b)
    vc = v.reshape(B, S // CHUNK, CHUNK, N_GMLP_HEADS, GMLP_HEAD_DIM)
    mixed = jnp.einsum('hpq,bcqhd->bcphd', w_s, vc) + b_s.T[None, None, :, :, None]
    return u * mixed.reshape(B, S, GMLP_WIDTH)


def conv_glu_ffn(x, w_up, conv_w, conv_b, w_down):
    h = x @ w_up
    hp = jnp.pad(h, ((0, 0), (1, 1), (0, 0)))
    h = hp[:, :-2] * conv_w[0] + hp[:, 1:-1] * conv_w[1] + hp[:, 2:] * conv_w[2] + conv_b
    g, u = jnp.split(h, 2, axis=-1)
    return (jax.nn.silu(g) * u) @ w_down


def setup_inputs(seed: int = 0) -> dict:
    key = jax.random.key(seed)
    ks = jax.random.split(key, 24)
    f32 = jnp.float32
    nrm = lambda k, shape, s: jax.random.normal(k, shape, f32) * s
    gain = lambda k, shape: 1.0 + 0.05 * jax.random.normal(k, shape, f32)
    L = DEPTH
    x = jax.random.normal(ks[0], (BATCH, SEQ, D_MODEL), f32)
    p = jax.random.normal(ks[1], (DEPTH, BATCH, SEQ, PLE_DIM), f32)
    positions = jnp.broadcast_to(jnp.arange(SEQ, dtype=jnp.int32), (BATCH, SEQ))
    return {
        "x": x,
        "p": p,
        "positions": positions,
        "g_mix": gain(ks[2], (L, D_MODEL)),
        "w_in": nrm(ks[3], (L, D_MODEL, IN_WIDTH), D_MODEL ** -0.5),
        "lambda_q1": nrm(ks[4], (L, DIFF_HALF), 0.1),
        "lambda_k1": nrm(ks[5], (L, DIFF_HALF), 0.1),
        "lambda_q2": nrm(ks[6], (L, DIFF_HALF), 0.1),
        "lambda_k2": nrm(ks[7], (L, DIFF_HALF), 0.1),
        "g_subln": gain(ks[8], (L, ATTN_HEAD_DIM)),
        "gmlp_ln_g": gain(ks[9], (L, GMLP_WIDTH)),
        "gmlp_ln_b": nrm(ks[10], (L, GMLP_WIDTH), 0.02),
        "w_spatial": nrm(ks[11], (L, N_GMLP_HEADS, CHUNK, CHUNK), CHUNK ** -0.5),
        "b_spatial": gain(ks[12], (L, N_GMLP_HEADS, CHUNK)),
        "w_out": nrm(ks[13], (L, D_MIX, D_MODEL), D_MIX ** -0.5),
        "g_ffn": gain(ks[14], (L, D_MODEL)),
        "w_up": nrm(ks[15], (L, D_MODEL, 2 * D_FF), D_MODEL ** -0.5),
        "conv_w": nrm(ks[16], (L, CONV_WIDTH, 2 * D_FF), CONV_WIDTH ** -0.5),
        "conv_b": nrm(ks[17], (L, 2 * D_FF), 0.02),
        "w_down": nrm(ks[18], (L, D_FF, D_MODEL), D_FF ** -0.5),
        "g_ple": gain(ks[19], (L, D_MODEL)),
        "w_ple_gate": nrm(ks[20], (L, D_MODEL, D_MODEL), D_MODEL ** -0.5),
        "w_ple_up": nrm(ks[21], (L, PLE_DIM, D_MODEL), PLE_DIM ** -0.5),
        "g_final": gain(ks[22], (D_MODEL,)),
    }


def reference(x, p, positions, g_mix, w_in, lambda_q1, lambda_k1, lambda_q2, lambda_k2,
              g_subln, gmlp_ln_g, gmlp_ln_b, w_spatial, b_spatial, w_out, g_ffn,
              w_up, conv_w, conv_b, w_down, g_ple, w_ple_gate, w_ple_up, g_final):
    B, S = x.shape[0], x.shape[1]
    cos, sin = rope_tables(positions, x.dtype)
    h = x
    for i in range(DEPTH):
        a = rms_norm(h, g_mix[i])
        z = a @ w_in[i]
        o0, o1, o2, o3 = QK_WIDTH, 2 * QK_WIDTH, 2 * QK_WIDTH + ATTN_WIDTH, 2 * QK_WIDTH + ATTN_WIDTH + GMLP_WIDTH
        q = z[..., :o0].reshape(B, S, N_ATTN_HEADS, 2, DIFF_HALF)
        k = z[..., o0:o1].reshape(B, S, N_ATTN_HEADS, 2, DIFF_HALF)
        v = z[..., o1:o2].reshape(B, S, N_ATTN_HEADS, ATTN_HEAD_DIM)
        gu = z[..., o2:o3]
        gv = z[..., o3:]
        q = apply_partial_rope(q, cos, sin)
        k = apply_partial_rope(k, cos, sin)
        lambda_init = 0.8 - 0.6 * math.exp(-0.3 * i)
        lam = (jnp.exp(jnp.sum(lambda_q1[i].astype(jnp.float32) * lambda_k1[i].astype(jnp.float32)))
               - jnp.exp(jnp.sum(lambda_q2[i].astype(jnp.float32) * lambda_k2[i].astype(jnp.float32)))
               + lambda_init)
        attn_out = diff_attention(q, k, v, lam, lambda_init, g_subln[i])
        gmlp_out = spatial_gating(gu, gv, gmlp_ln_g[i], gmlp_ln_b[i],
                                  w_spatial[i], b_spatial[i])
        mix = jnp.concatenate([attn_out, gmlp_out], axis=-1)
        h = h + mix @ w_out[i]
        h = h + conv_glu_ffn(rms_norm(h, g_ffn[i]), w_up[i], conv_w[i], conv_b[i], w_down[i])
        gate = jax.nn.sigmoid(rms_norm(h, g_ple[i]) @ w_ple_gate[i])
        h = h + (p[i] @ w_ple_up[i]) * gate
    return rms_norm(h, g_final)
```

```python
import functools
import math

import jax
import jax.numpy as jnp
from jax import lax
from jax.experimental import pallas as pl
from jax.experimental.pallas import tpu as pltpu

D_MODEL = 2048
DEPTH = 1
ATTN_WIDTH = D_MODEL // 2
GMLP_WIDTH = D_MODEL - ATTN_WIDTH
ATTN_HEAD_DIM = 128
N_ATTN_HEADS = ATTN_WIDTH // ATTN_HEAD_DIM
DIFF_HALF = ATTN_HEAD_DIM // 2
ROPE_DIM = DIFF_HALF // 4
ROPE_HALF = ROPE_DIM // 2
ROPE_THETA = 500000.0
N_GMLP_HEADS = 8
GMLP_HEAD_DIM = GMLP_WIDTH // N_GMLP_HEADS
CHUNK = 128
QK_WIDTH = N_ATTN_HEADS * 2 * DIFF_HALF
IN_WIDTH = 2 * QK_WIDTH + ATTN_WIDTH + 2 * GMLP_WIDTH
D_FF = 5632
PLE_DIM = 256
EPS = 1e-6

LANES = 128
BF16_ROWS = 16
VMEM_LIMIT = 52 * 1024 * 1024

F32 = jnp.float32
BF16 = jnp.bfloat16


def _params(n_axes):
    return pltpu.CompilerParams(
        dimension_semantics=("arbitrary",) * n_axes, vmem_limit_bytes=VMEM_LIMIT)


def _rms(x, g):
    return x * lax.rsqrt(jnp.mean(x * x, axis=-1, keepdims=True) + EPS) * g


def _gelu(x):
    return 0.5 * x * (1.0 + lax.erf(x * (2.0 ** -0.5)))


def _sigmoid(x):
    return 1.0 / (1.0 + jnp.exp(-x))


def _in_proj_kernel(x_ref, pos_ref, freq_ref, g_ref, w_ref, z_ref,
                    a_sc, cos_sc, sin_up_sc, sin_dn_sc, *, tn):
    j = pl.program_id(1)
    n_q_tiles = QK_WIDTH // tn
    n_rope_tiles = 2 * QK_WIDTH // tn

    @pl.when(j == 0)
    def _():
        a_sc[...] = _rms(x_ref[...], g_ref[...]).astype(BF16)
        lane = lax.broadcasted_iota(jnp.int32, (1, LANES), 1)
        d = lane % DIFF_HALF
        ang = pos_ref[...].astype(F32) * freq_ref[...]
        c = jnp.cos(ang)
        s = jnp.sin(ang)
        cos_sc[...] = jnp.where(d < ROPE_DIM, c, 1.0)
        sin_up_sc[...] = jnp.where(d < ROPE_HALF, -s, 0.0)
        sin_dn_sc[...] = jnp.where((d >= ROPE_HALF) & (d < ROPE_DIM), s, 0.0)

    acc = jnp.dot(a_sc[...], w_ref[...], preferred_element_type=F32)

    @pl.when(j < n_rope_tiles)
    def _():
        scale = jnp.where(j < n_q_tiles, DIFF_HALF ** -0.5, 1.0).astype(F32)
        for c0 in range(0, tn, LANES):
            t = acc[:, c0:c0 + LANES]
            r = (t * cos_sc[...]
                 + pltpu.roll(t, LANES - ROPE_HALF, 1) * sin_up_sc[...]
                 + pltpu.roll(t, ROPE_HALF, 1) * sin_dn_sc[...])
            z_ref[:, c0:c0 + LANES] = (r * scale).astype(BF16)

    @pl.when(j >= n_rope_tiles)
    def _():
        z_ref[...] = acc.astype(BF16)


def _in_proj(x2, pos2, freq, g, w, *, tm=1024, tn=512):
    m = x2.shape[0]
    return pl.pallas_call(
        functools.partial(_in_proj_kernel, tn=tn),
        grid=(m // tm, IN_WIDTH // tn),
        in_specs=[
            pl.BlockSpec((tm, D_MODEL), lambda i, j: (i, 0)),
            pl.BlockSpec((tm, 1), lambda i, j: (i, 0)),
            pl.BlockSpec((1, LANES), lambda i, j: (0, 0)),
            pl.BlockSpec((1, D_MODEL), lambda i, j: (0, 0)),
            pl.BlockSpec((D_MODEL, tn), lambda i, j: (0, j)),
        ],
        out_specs=pl.BlockSpec((tm, tn), lambda i, j: (i, j)),
        out_shape=jax.ShapeDtypeStruct((m, IN_WIDTH), BF16),
        scratch_shapes=[
            pltpu.VMEM((tm, D_MODEL), BF16),
            pltpu.VMEM((tm, LANES), F32),
            pltpu.VMEM((tm, LANES), F32),
            pltpu.VMEM((tm, LANES), F32),
        ],
        compiler_params=_params(2),
        name="in_proj",
    )(x2, pos2, freq, g, w)


def _diff_attn_kernel(q_ref, k_ref, v_ref, lq1_ref, lk1_ref, lq2_ref, lk2_ref, g_ref,
                      o_ref, vext_sc, *, lambda_init):
    i = pl.program_id(2)

    @pl.when(i == 0)
    def _():
        vext_sc[:, :ATTN_HEAD_DIM] = v_ref[...]
        vext_sc[:, ATTN_HEAD_DIM:] = jnp.ones((v_ref.shape[0], ATTN_HEAD_DIM), BF16)

    lam = (jnp.exp(jnp.sum(lq1_ref[...] * lk1_ref[...], axis=-1, keepdims=True))
           - jnp.exp(jnp.sum(lq2_ref[...] * lk2_ref[...], axis=-1, keepdims=True))
           + lambda_init)

    q = q_ref[...]
    k = k_ref[...]
    lane = lax.broadcasted_iota(jnp.int32, (1, LANES), 1)
    nt = (((1,), (1,)), ((), ()))

    def component(qc):
        s = lax.dot_general(qc, k, nt, preferred_element_type=F32)
        e = jnp.exp(s - jnp.max(s, axis=-1, keepdims=True)).astype(BF16)
        oe = jnp.dot(e, vext_sc[...], preferred_element_type=F32)
        return oe[:, :ATTN_HEAD_DIM] / oe[:, ATTN_HEAD_DIM:ATTN_HEAD_DIM + 1]

    o1 = component(jnp.where(lane < DIFF_HALF, q, jnp.zeros_like(q)))
    o2 = component(jnp.where(lane >= DIFF_HALF, q, jnp.zeros_like(q)))
    o = o1 - lam * o2
    o_ref[...] = (_rms(o, g_ref[...]) * (1.0 - lambda_init)).astype(BF16)


def _diff_attn(z, lq1, lk1, lq2, lk2, g, *, batch, seq, lambda_init, tq=256):
    nq = seq // tq
    kv_spec = lambda col0: pl.BlockSpec(
        (seq, ATTN_HEAD_DIM), lambda b, h, i: (b, col0 + h))
    vec_spec = lambda n: pl.BlockSpec((1, n), lambda b, h, i: (0, 0))
    return pl.pallas_call(
        functools.partial(_diff_attn_kernel, lambda_init=lambda_init),
        grid=(batch, N_ATTN_HEADS, nq),
        in_specs=[
            pl.BlockSpec((tq, ATTN_HEAD_DIM), lambda b, h, i: (b * nq + i, h)),
            kv_spec(QK_WIDTH // ATTN_HEAD_DIM),
            kv_spec(2 * QK_WIDTH // ATTN_HEAD_DIM),
            vec_spec(DIFF_HALF), vec_spec(DIFF_HALF), vec_spec(DIFF_HALF), vec_spec(DIFF_HALF),
            vec_spec(ATTN_HEAD_DIM),
        ],
        out_specs=pl.BlockSpec((tq, ATTN_HEAD_DIM), lambda b, h, i: (b * nq + i, h)),
        out_shape=jax.ShapeDtypeStruct((batch * seq, ATTN_WIDTH), BF16),
        scratch_shapes=[pltpu.VMEM((seq, 2 * ATTN_HEAD_DIM), BF16)],
        compiler_params=_params(3),
        name="diff_attn",
    )(z, z, z, lq1, lk1, lq2, lk2, g)


def _gmlp_kernel(u_ref, v_ref, lng_ref, lnb_ref, ws_ref, bs_ref, o_ref, *, rows):
    v = _gelu(v_ref[...].astype(F32))
    mu = jnp.mean(v, axis=-1, keepdims=True)
    vc = v - mu
    var = jnp.mean(vc * vc, axis=-1, keepdims=True)
    vn = (vc * lax.rsqrt(var + EPS) * lng_ref[...] + lnb_ref[...]).astype(BF16)
    for c0 in range(0, rows, CHUNK):
        for h in range(N_GMLP_HEADS):
            cols = slice(h * GMLP_HEAD_DIM, (h + 1) * GMLP_HEAD_DIM)
            mixed = jnp.dot(ws_ref[h], vn[c0:c0 + CHUNK, cols], preferred_element_type=F32)
            mixed = mixed + bs_ref[:, h:h + 1]
            u = _gelu(u_ref[c0:c0 + CHUNK, cols].astype(F32))
            o_ref[c0:c0 + CHUNK, cols] = (u * mixed).astype(BF16)


def _gmlp(z, ln_g, ln_b, ws, bs_t, *, rows=512):
    m = z.shape[0]
    u_blk = (2 * QK_WIDTH + ATTN_WIDTH) // GMLP_WIDTH
    const2 = lambda i: (0, 0)
    return pl.pallas_call(
        functools.partial(_gmlp_kernel, rows=rows),
        grid=(m // rows,),
        in_specs=[
            pl.BlockSpec((rows, GMLP_WIDTH), lambda i: (i, u_blk)),
            pl.BlockSpec((rows, GMLP_WIDTH), lambda i: (i, u_blk + 1)),
            pl.BlockSpec((1, GMLP_WIDTH), const2),
            pl.BlockSpec((1, GMLP_WIDTH), const2),
            pl.BlockSpec((N_GMLP_HEADS, CHUNK, CHUNK), lambda i: (0, 0, 0)),
            pl.BlockSpec((CHUNK, N_GMLP_HEADS), const2),
        ],
        out_specs=pl.BlockSpec((rows, GMLP_WIDTH), lambda i: (i, 0)),
        out_shape=jax.ShapeDtypeStruct((m, GMLP_WIDTH), BF16),
        compiler_params=_params(1),
        name="gmlp",
    )(z, z, ln_g, ln_b, ws, bs_t)


def _out_proj_kernel(attn_ref, gm_ref, x_ref, w_ref, g_ref, h_ref, a_ref):
    acc = jnp.dot(attn_ref[...], w_ref[:ATTN_WIDTH, :], preferred_element_type=F32)
    acc = acc + jnp.dot(gm_ref[...], w_ref[ATTN_WIDTH:, :], preferred_element_type=F32)
    h = x_ref[...] + acc
    h_ref[...] = h
    a_ref[...] = _rms(h, g_ref[...]).astype(BF16)


def _out_proj(attn, gm, x2, w, g, *, tm=256):
    m = x2.shape[0]
    row = lambda n: pl.BlockSpec((tm, n), lambda i: (i, 0))
    return pl.pallas_call(
        _out_proj_kernel,
        grid=(m // tm,),
        in_specs=[
            row(ATTN_WIDTH), row(GMLP_WIDTH), row(D_MODEL),
            pl.BlockSpec((D_MODEL, D_MODEL), lambda i: (0, 0)),
            pl.BlockSpec((1, D_MODEL), lambda i: (0, 0)),
        ],
        out_specs=[row(D_MODEL), row(D_MODEL)],
        out_shape=[jax.ShapeDtypeStruct((m, D_MODEL), F32),
                   jax.ShapeDtypeStruct((m, D_MODEL), BF16)],
        compiler_params=_params(1),
        name="out_proj",
    )(attn, gm, x2, w, g)


def _ffn_up_kernel(a_ref, prev_ref, next_ref, wg_ref, wu_ref, cwg_ref, cwu_ref,
                   cbg_ref, cbu_ref, o_ref, ext_sc, h_sc, *, tm, seq):
    i = pl.program_id(0)
    j = pl.program_id(1)
    halo = BF16_ROWS

    @pl.when(j == 0)
    def _():
        first = (i * tm) % seq == 0
        last = ((i + 1) * tm) % seq == 0
        ext_sc[halo:halo + tm, :] = a_ref[...]
        ext_sc[:halo, :] = jnp.where(first, jnp.zeros_like(prev_ref[...]), prev_ref[...])
        ext_sc[halo + tm:, :] = jnp.where(last, jnp.zeros_like(next_ref[...]), next_ref[...])

    def conv_half(w_ref, cw_ref, cb_ref):
        h_sc[...] = jnp.dot(ext_sc[...], w_ref[...], preferred_element_type=F32)
        return (h_sc[halo - 1:halo - 1 + tm, :] * cw_ref[0:1, :]
                + h_sc[halo:halo + tm, :] * cw_ref[1:2, :]
                + h_sc[halo + 1:halo + 1 + tm, :] * cw_ref[2:3, :]
                + cb_ref[...])

    g = conv_half(wg_ref, cwg_ref, cbg_ref)
    u = conv_half(wu_ref, cwu_ref, cbu_ref)
    o_ref[...] = (g * _sigmoid(g) * u).astype(BF16)


def _ffn_up(a2, w_up, conv_w, conv_b, *, seq, tm=512, tn=512):
    m = a2.shape[0]
    nj = D_FF // tn
    halo = BF16_ROWS
    hb = tm // halo
    n_hb = m // halo
    return pl.pallas_call(
        functools.partial(_ffn_up_kernel, tm=tm, seq=seq),
        grid=(m // tm, nj),
        in_specs=[
            pl.BlockSpec((tm, D_MODEL), lambda i, j: (i, 0)),
            pl.BlockSpec((halo, D_MODEL), lambda i, j: (jnp.maximum(i * hb - 1, 0), 0)),
            pl.BlockSpec((halo, D_MODEL), lambda i, j: (jnp.minimum((i + 1) * hb, n_hb - 1), 0)),
            pl.BlockSpec((D_MODEL, tn), lambda i, j: (0, j)),
            pl.BlockSpec((D_MODEL, tn), lambda i, j: (0, nj + j)),
            pl.BlockSpec((3, tn), lambda i, j: (0, j)),
            pl.BlockSpec((3, tn), lambda i, j: (0, nj + j)),
            pl.BlockSpec((1, tn), lambda i, j: (0, j)),
            pl.BlockSpec((1, tn), lambda i, j: (0, nj + j)),
        ],
        out_specs=pl.BlockSpec((tm, tn), lambda i, j: (i, j)),
        out_shape=jax.ShapeDtypeStruct((m, D_FF), BF16),
        scratch_shapes=[
            pltpu.VMEM((tm + 2 * halo, D_MODEL), BF16),
            pltpu.VMEM((tm + 2 * halo, tn), F32),
        ],
        compiler_params=_params(2),
        name="ffn_up",
    )(a2, a2, a2, w_up, w_up, conv_w, conv_w, conv_b, conv_b)


def _ffn_down_kernel(act_ref, w_ref, h1_ref, g_ref, h2_ref, a3_ref, acc_sc):
    k = pl.program_id(1)

    @pl.when(k == 0)
    def _():
        acc_sc[...] = h1_ref[...]

    acc_sc[...] += jnp.dot(act_ref[...], w_ref[...], preferred_element_type=F32)

    @pl.when(k == pl.num_programs(1) - 1)
    def _():
        h2 = acc_sc[...]
        h2_ref[...] = h2
        a3_ref[...] = _rms(h2, g_ref[...]).astype(BF16)


def _ffn_down(act, w, h1, g, *, tm=512, tk=512):
    m = act.shape[0]
    row = pl.BlockSpec((tm, D_MODEL), lambda i, k: (i, 0))
    return pl.pallas_call(
        _ffn_down_kernel,
        grid=(m // tm, D_FF // tk),
        in_specs=[
            pl.BlockSpec((tm, tk), lambda i, k: (i, k)),
            pl.BlockSpec((tk, D_MODEL), lambda i, k: (k, 0)),
            row,
            pl.BlockSpec((1, D_MODEL), lambda i, k: (0, 0)),
        ],
        out_specs=[row, row],
        out_shape=[jax.ShapeDtypeStruct((m, D_MODEL), F32),
                   jax.ShapeDtypeStruct((m, D_MODEL), BF16)],
        scratch_shapes=[pltpu.VMEM((tm, D_MODEL), F32)],
        compiler_params=_params(2),
        name="ffn_down",
    )(act, w, h1, g)


def _ple_kernel(a3_ref, h2_ref, p_ref, wg_ref, wu_ref, gn_ref, o_ref, *, final):
    gate = _sigmoid(jnp.dot(a3_ref[...], wg_ref[...], preferred_element_type=F32))
    pe = jnp.dot(p_ref[...].astype(BF16), wu_ref[...], preferred_element_type=F32)
    h3 = h2_ref[...] + pe * gate
    o_ref[...] = _rms(h3, gn_ref[...]) if final else h3


def _ple(a3, h2, p2, w_gate, w_up, g_norm, *, final, tm=256):
    m = a3.shape[0]
    row = lambda n: pl.BlockSpec((tm, n), lambda i: (i, 0))
    return pl.pallas_call(
        functools.partial(_ple_kernel, final=final),
        grid=(m // tm,),
        in_specs=[
            row(D_MODEL), row(D_MODEL), row(PLE_DIM),
            pl.BlockSpec((D_MODEL, D_MODEL), lambda i: (0, 0)),
            pl.BlockSpec((PLE_DIM, D_MODEL), lambda i: (0, 0)),
            pl.BlockSpec((1, D_MODEL), lambda i: (0, 0)),
        ],
        out_specs=row(D_MODEL),
        out_shape=jax.ShapeDtypeStruct((m, D_MODEL), F32),
        compiler_params=_params(1),
        name="ple",
    )(a3, h2, p2, w_gate, w_up, g_norm)


def kernel(x, p, positions, g_mix, w_in, lambda_q1, lambda_k1, lambda_q2, lambda_k2, g_subln, gmlp_ln_g, gmlp_ln_b, w_spatial, b_spatial, w_out, g_ffn, w_up, conv_w, conv_b, w_down, g_ple, w_ple_gate, w_ple_up, g_final):
    batch, seq, _ = x.shape
    m = batch * seq
    row = lambda v: v.reshape(1, -1).astype(F32)

    inv_freq = ROPE_THETA ** (-jnp.arange(0, ROPE_DIM, 2, dtype=F32) / ROPE_DIM)
    freq = jnp.tile(inv_freq, LANES // ROPE_HALF).reshape(1, LANES)
    pos2 = positions.reshape(m, 1)

    h = x.reshape(m, D_MODEL)
    for i in range(DEPTH):
        lambda_init = 0.8 - 0.6 * math.exp(-0.3 * i)
        z = _in_proj(h, pos2, freq, row(g_mix[i]), w_in[i].astype(BF16))
        attn = _diff_attn(z, row(lambda_q1[i]), row(lambda_k1[i]), row(lambda_q2[i]),
                          row(lambda_k2[i]), row(g_subln[i]),
                          batch=batch, seq=seq, lambda_init=lambda_init)
        gm = _gmlp(z, row(gmlp_ln_g[i]), row(gmlp_ln_b[i]),
                   w_spatial[i].astype(BF16), b_spatial[i].T.astype(F32))
        h1, a2 = _out_proj(attn, gm, h, w_out[i].astype(BF16), row(g_ffn[i]))
        act = _ffn_up(a2, w_up[i].astype(BF16), conv_w[i].astype(F32), row(conv_b[i]), seq=seq)
        h2, a3 = _ffn_down(act, w_down[i].astype(BF16), h1, row(g_ple[i]))
        final = i == DEPTH - 1
        h = _ple(a3, h2, p[i].reshape(m, PLE_DIM), w_ple_gate[i].astype(BF16),
                 w_ple_up[i].astype(BF16), row(g_final), final=final)
    return h.reshape(batch, seq, D_MODEL)
```

```python
import functools
import math

import jax
import jax.numpy as jnp
from jax import lax
from jax.experimental import pallas as pl
from jax.experimental.pallas import tpu as pltpu

D_MODEL = 2048
DEPTH = 1
ATTN_WIDTH = D_MODEL // 2
GMLP_WIDTH = D_MODEL - ATTN_WIDTH
ATTN_HEAD_DIM = 128
N_ATTN_HEADS = ATTN_WIDTH // ATTN_HEAD_DIM
DIFF_HALF = ATTN_HEAD_DIM // 2
ROPE_DIM = DIFF_HALF // 4
ROPE_HALF = ROPE_DIM // 2
ROPE_THETA = 500000.0
N_GMLP_HEADS = 8
GMLP_HEAD_DIM = GMLP_WIDTH // N_GMLP_HEADS
CHUNK = 128
QK_WIDTH = N_ATTN_HEADS * 2 * DIFF_HALF
IN_WIDTH = 2 * QK_WIDTH + ATTN_WIDTH + 2 * GMLP_WIDTH
D_FF = 5632
PLE_DIM = 256
EPS = 1e-6

LANES = 128
BF16_ROWS = 16
VMEM_LIMIT = 52 * 1024 * 1024

F32 = jnp.float32
BF16 = jnp.bfloat16


def _params(n_axes):
    return pltpu.CompilerParams(
        dimension_semantics=("arbitrary",) * n_axes, vmem_limit_bytes=VMEM_LIMIT)


def _rms(x, g):
    return x * lax.rsqrt(jnp.mean(x * x, axis=-1, keepdims=True) + EPS) * g


def _gelu(x):
    return 0.5 * x * (1.0 + lax.erf(x * (2.0 ** -0.5)))


def _sigmoid(x):
    return 1.0 / (1.0 + jnp.exp(-x))


def _in_proj_kernel(x_ref, pos_ref, freq_ref, g_ref, w_ref, z_ref,
                    a_sc, cos_sc, sin_up_sc, sin_dn_sc, *, tn):
    j = pl.program_id(1)
    n_q_tiles = QK_WIDTH // tn
    n_rope_tiles = 2 * QK_WIDTH // tn

    @pl.when(j == 0)
    def _():
        a_sc[...] = _rms(x_ref[...], g_ref[...]).astype(BF16)
        lane = lax.broadcasted_iota(jnp.int32, (1, LANES), 1)
        d = lane % DIFF_HALF
        ang = pos_ref[...].astype(F32) * freq_ref[...]
        c = jnp.cos(ang)
        s = jnp.sin(ang)
        cos_sc[...] = jnp.where(d < ROPE_DIM, c, 1.0)
        sin_up_sc[...] = jnp.where(d < ROPE_HALF, -s, 0.0)
        sin_dn_sc[...] = jnp.where((d >= ROPE_HALF) & (d < ROPE_DIM), s, 0.0)

    @pl.when(j < n_rope_tiles)
    def _():
        acc = jnp.dot(a_sc[...], w_ref[...], preferred_element_type=F32)
        scale = jnp.where(j < n_q_tiles, DIFF_HALF ** -0.5 * math.log2(math.e), 1.0).astype(F32)
        for c0 in range(0, tn, LANES):
            t = acc[:, c0:c0 + LANES]
            r = (t * cos_sc[...]
                 + pltpu.roll(t, LANES - ROPE_HALF, 1) * sin_up_sc[...]
                 + pltpu.roll(t, ROPE_HALF, 1) * sin_dn_sc[...])
            z_ref[:, c0:c0 + LANES] = (r * scale).astype(BF16)

    @pl.when(j >= n_rope_tiles)
    def _():
        z_ref[...] = jnp.dot(a_sc[...], w_ref[...], preferred_element_type=F32).astype(BF16)


def _in_proj(x2, pos2, freq, g, w, *, tm=1024, tn=512):
    m = x2.shape[0]
    return pl.pallas_call(
        functools.partial(_in_proj_kernel, tn=tn),
        grid=(m // tm, IN_WIDTH // tn),
        in_specs=[
            pl.BlockSpec((tm, D_MODEL), lambda i, j: (i, 0)),
            pl.BlockSpec((tm, 1), lambda i, j: (i, 0)),
            pl.BlockSpec((1, LANES), lambda i, j: (0, 0)),
            pl.BlockSpec((1, D_MODEL), lambda i, j: (0, 0)),
            pl.BlockSpec((D_MODEL, tn), lambda i, j: (0, j)),
        ],
        out_specs=pl.BlockSpec((tm, tn), lambda i, j: (i, j)),
        out_shape=jax.ShapeDtypeStruct((m, IN_WIDTH), BF16),
        scratch_shapes=[
            pltpu.VMEM((tm, D_MODEL), BF16),
            pltpu.VMEM((tm, LANES), F32),
            pltpu.VMEM((tm, LANES), F32),
            pltpu.VMEM((tm, LANES), F32),
        ],
        compiler_params=_params(2),
        name="in_proj",
    )(x2, pos2, freq, g, w)


def _diff_attn_kernel(q_ref, k_ref, v_ref, lq1_ref, lk1_ref, lq2_ref, lk2_ref, g_ref,
                      o_ref, vext_sc, *, lambda_init, sub):
    seq = q_ref.shape[0]
    vext_sc[:, :ATTN_HEAD_DIM] = v_ref[...]
    vext_sc[:, ATTN_HEAD_DIM:] = jnp.ones((seq, ATTN_HEAD_DIM), BF16)

    lam = (jnp.exp(jnp.sum(lq1_ref[...] * lk1_ref[...], axis=-1, keepdims=True))
           - jnp.exp(jnp.sum(lq2_ref[...] * lk2_ref[...], axis=-1, keepdims=True))
           + lambda_init)

    k = k_ref[...]
    lane = lax.broadcasted_iota(jnp.int32, (1, LANES), 1)
    nt = (((1,), (1,)), ((), ()))

    def component(qc):
        s = lax.dot_general(qc, k, nt, preferred_element_type=F32)
        e = jnp.exp2(s - jnp.max(s, axis=-1, keepdims=True)).astype(BF16)
        oe = jnp.dot(e, vext_sc[...], preferred_element_type=F32)
        return oe[:, :ATTN_HEAD_DIM] / oe[:, ATTN_HEAD_DIM:ATTN_HEAD_DIM + 1]

    for r0 in range(0, seq, sub):
        q = q_ref[r0:r0 + sub, :]
        o1 = component(jnp.where(lane < DIFF_HALF, q, jnp.zeros_like(q)))
        o2 = component(jnp.where(lane >= DIFF_HALF, q, jnp.zeros_like(q)))
        o = o1 - lam * o2
        o_ref[r0:r0 + sub, :] = (_rms(o, g_ref[...]) * (1.0 - lambda_init)).astype(BF16)


def _diff_attn(z, lq1, lk1, lq2, lk2, g, *, batch, seq, lambda_init, sub=256):
    head_blk = lambda col0: pl.BlockSpec((seq, ATTN_HEAD_DIM), lambda b, h: (b, col0 + h))
    vec_spec = lambda n: pl.BlockSpec((1, n), lambda b, h: (0, 0))
    return pl.pallas_call(
        functools.partial(_diff_attn_kernel, lambda_init=lambda_init, sub=sub),
        grid=(batch, N_ATTN_HEADS),
        in_specs=[
            head_blk(0),
            head_blk(QK_WIDTH // ATTN_HEAD_DIM),
            head_blk(2 * QK_WIDTH // ATTN_HEAD_DIM),
            vec_spec(DIFF_HALF), vec_spec(DIFF_HALF), vec_spec(DIFF_HALF), vec_spec(DIFF_HALF),
            vec_spec(ATTN_HEAD_DIM),
        ],
        out_specs=head_blk(0),
        out_shape=jax.ShapeDtypeStruct((batch * seq, ATTN_WIDTH), BF16),
        scratch_shapes=[pltpu.VMEM((seq, 2 * ATTN_HEAD_DIM), BF16)],
        compiler_params=_params(2),
        name="diff_attn",
    )(z, z, z, lq1, lk1, lq2, lk2, g)


def _gmlp_kernel(u_ref, v_ref, lng_ref, lnb_ref, ws_ref, bs_ref, o_ref, *, rows):
    v = _gelu(v_ref[...].astype(F32))
    mu = jnp.mean(v, axis=-1, keepdims=True)
    vc = v - mu
    var = jnp.mean(vc * vc, axis=-1, keepdims=True)
    vn = (vc * lax.rsqrt(var + EPS) * lng_ref[...] + lnb_ref[...]).astype(BF16)
    for c0 in range(0, rows, CHUNK):
        for h in range(N_GMLP_HEADS):
            cols = slice(h * GMLP_HEAD_DIM, (h + 1) * GMLP_HEAD_DIM)
            mixed = jnp.dot(ws_ref[h], vn[c0:c0 + CHUNK, cols], preferred_element_type=F32)
            mixed = mixed + bs_ref[:, h:h + 1]
            u = _gelu(u_ref[c0:c0 + CHUNK, cols].astype(F32))
            o_ref[c0:c0 + CHUNK, cols] = (u * mixed).astype(BF16)


def _gmlp(z, ln_g, ln_b, ws, bs_t, *, rows=512):
    m = z.shape[0]
    u_blk = (2 * QK_WIDTH + ATTN_WIDTH) // GMLP_WIDTH
    const2 = lambda i: (0, 0)
    return pl.pallas_call(
        functools.partial(_gmlp_kernel, rows=rows),
        grid=(m // rows,),
        in_specs=[
            pl.BlockSpec((rows, GMLP_WIDTH), lambda i: (i, u_blk)),
            pl.BlockSpec((rows, GMLP_WIDTH), lambda i: (i, u_blk + 1)),
            pl.BlockSpec((1, GMLP_WIDTH), const2),
            pl.BlockSpec((1, GMLP_WIDTH), const2),
            pl.BlockSpec((N_GMLP_HEADS, CHUNK, CHUNK), lambda i: (0, 0, 0)),
            pl.BlockSpec((CHUNK, N_GMLP_HEADS), const2),
        ],
        out_specs=pl.BlockSpec((rows, GMLP_WIDTH), lambda i: (i, 0)),
        out_shape=jax.ShapeDtypeStruct((m, GMLP_WIDTH), BF16),
        compiler_params=_params(1),
        name="gmlp",
    )(z, z, ln_g, ln_b, ws, bs_t)


def _out_proj_kernel(attn_ref, gm_ref, x_ref, w_ref, g_ref, h_ref, a_ref):
    acc = jnp.dot(attn_ref[...], w_ref[:ATTN_WIDTH, :], preferred_element_type=F32)
    acc = acc + jnp.dot(gm_ref[...], w_ref[ATTN_WIDTH:, :], preferred_element_type=F32)
    h = x_ref[...] + acc
    h_ref[...] = h
    a_ref[...] = _rms(h, g_ref[...]).astype(BF16)


def _out_proj(attn, gm, x2, w, g, *, tm=256):
    m = x2.shape[0]
    row = lambda n: pl.BlockSpec((tm, n), lambda i: (i, 0))
    return pl.pallas_call(
        _out_proj_kernel,
        grid=(m // tm,),
        in_specs=[
            row(ATTN_WIDTH), row(GMLP_WIDTH), row(D_MODEL),
            pl.BlockSpec((D_MODEL, D_MODEL), lambda i: (0, 0), pipeline_mode=pl.Buffered(1)),
            pl.BlockSpec((1, D_MODEL), lambda i: (0, 0)),
        ],
        out_specs=[row(D_MODEL), row(D_MODEL)],
        out_shape=[jax.ShapeDtypeStruct((m, D_MODEL), F32),
                   jax.ShapeDtypeStruct((m, D_MODEL), BF16)],
        compiler_params=_params(1),
        name="out_proj",
    )(attn, gm, x2, w, g)


def _ffn_up_kernel(a_ref, prev_ref, next_ref, wg_ref, wu_ref, cwg_ref, cwu_ref,
                   cbg_ref, cbu_ref, o_ref, ext_sc, h_sc, *, tm, seq):
    i = pl.program_id(0)
    j = pl.program_id(1)
    halo = BF16_ROWS

    @pl.when(j == 0)
    def _():
        first = (i * tm) % seq == 0
        last = ((i + 1) * tm) % seq == 0
        ext_sc[halo:halo + tm, :] = a_ref[...]
        ext_sc[:halo, :] = jnp.where(first, jnp.zeros_like(prev_ref[...]), prev_ref[...])
        ext_sc[halo + tm:, :] = jnp.where(last, jnp.zeros_like(next_ref[...]), next_ref[...])

    def conv_half(w_ref, cw_ref, cb_ref):
        h_sc[...] = jnp.dot(ext_sc[...], w_ref[...], preferred_element_type=F32)
        return (h_sc[halo - 1:halo - 1 + tm, :] * cw_ref[0:1, :]
                + h_sc[halo:halo + tm, :] * cw_ref[1:2, :]
                + h_sc[halo + 1:halo + 1 + tm, :] * cw_ref[2:3, :]
                + cb_ref[...])

    g = conv_half(wg_ref, cwg_ref, cbg_ref)
    u = conv_half(wu_ref, cwu_ref, cbu_ref)
    o_ref[...] = (g * _sigmoid(g) * u).astype(BF16)


def _ffn_up(a2, w_up, conv_w, conv_b, *, seq, tm=512, tn=512):
    m = a2.shape[0]
    nj = D_FF // tn
    halo = BF16_ROWS
    hb = tm // halo
    n_hb = m // halo
    return pl.pallas_call(
        functools.partial(_ffn_up_kernel, tm=tm, seq=seq),
        grid=(m // tm, nj),
        in_specs=[
            pl.BlockSpec((tm, D_MODEL), lambda i, j: (i, 0)),
            pl.BlockSpec((halo, D_MODEL), lambda i, j: (jnp.maximum(i * hb - 1, 0), 0)),
            pl.BlockSpec((halo, D_MODEL), lambda i, j: (jnp.minimum((i + 1) * hb, n_hb - 1), 0)),
            pl.BlockSpec((D_MODEL, tn), lambda i, j: (0, j)),
            pl.BlockSpec((D_MODEL, tn), lambda i, j: (0, nj + j)),
            pl.BlockSpec((3, tn), lambda i, j: (0, j)),
            pl.BlockSpec((3, tn), lambda i, j: (0, nj + j)),
            pl.BlockSpec((1, tn), lambda i, j: (0, j)),
            pl.BlockSpec((1, tn), lambda i, j: (0, nj + j)),
        ],
        out_specs=pl.BlockSpec((tm, tn), lambda i, j: (i, j)),
        out_shape=jax.ShapeDtypeStruct((m, D_FF), BF16),
        scratch_shapes=[
            pltpu.VMEM((tm + 2 * halo, D_MODEL), BF16),
            pltpu.VMEM((tm + 2 * halo, tn), F32),
        ],
        compiler_params=_params(2),
        name="ffn_up",
    )(a2, a2, a2, w_up, w_up, conv_w, conv_w, conv_b, conv_b)


def _ffn_down_kernel(act_ref, w_ref, h1_ref, g_ref, h2_ref, a3_ref):
    h2 = h1_ref[...] + jnp.dot(act_ref[...], w_ref[...], preferred_element_type=F32)
    h2_ref[...] = h2
    a3_ref[...] = _rms(h2, g_ref[...]).astype(BF16)


def _ffn_down(act, w, h1, g, *, tm=256):
    m = act.shape[0]
    row = pl.BlockSpec((tm, D_MODEL), lambda i: (i, 0))
    return pl.pallas_call(
        _ffn_down_kernel,
        grid=(m // tm,),
        in_specs=[
            pl.BlockSpec((tm, D_FF), lambda i: (i, 0)),
            pl.BlockSpec((D_FF, D_MODEL), lambda i: (0, 0), pipeline_mode=pl.Buffered(1)),
            row,
            pl.BlockSpec((1, D_MODEL), lambda i: (0, 0)),
        ],
        out_specs=[row, row],
        out_shape=[jax.ShapeDtypeStruct((m, D_MODEL), F32),
                   jax.ShapeDtypeStruct((m, D_MODEL), BF16)],
        compiler_params=_params(1),
        name="ffn_down",
    )(act, w, h1, g)


def _ple_kernel(a3_ref, h2_ref, p_ref, wg_ref, wu_ref, gn_ref, o_ref, *, final):
    gate = _sigmoid(jnp.dot(a3_ref[...], wg_ref[...], preferred_element_type=F32))
    pe = jnp.dot(p_ref[...].astype(BF16), wu_ref[...], preferred_element_type=F32)
    h3 = h2_ref[...] + pe * gate
    o_ref[...] = _rms(h3, gn_ref[...]) if final else h3


def _ple(a3, h2, p2, w_gate, w_up, g_norm, *, final, tm=256):
    m = a3.shape[0]
    row = lambda n: pl.BlockSpec((tm, n), lambda i: (i, 0))
    return pl.pallas_call(
        functools.partial(_ple_kernel, final=final),
        grid=(m // tm,),
        in_specs=[
            row(D_MODEL), row(D_MODEL), row(PLE_DIM),
            pl.BlockSpec((D_MODEL, D_MODEL), lambda i: (0, 0), pipeline_mode=pl.Buffered(1)),
            pl.BlockSpec((PLE_DIM, D_MODEL), lambda i: (0, 0)),
            pl.BlockSpec((1, D_MODEL), lambda i: (0, 0)),
        ],
        out_specs=row(D_MODEL),
        out_shape=jax.ShapeDtypeStruct((m, D_MODEL), F32),
        compiler_params=_params(1),
        name="ple",
    )(a3, h2, p2, w_gate, w_up, g_norm)


def kernel(x, p, positions, g_mix, w_in, lambda_q1, lambda_k1, lambda_q2, lambda_k2, g_subln, gmlp_ln_g, gmlp_ln_b, w_spatial, b_spatial, w_out, g_ffn, w_up, conv_w, conv_b, w_down, g_ple, w_ple_gate, w_ple_up, g_final):
    batch, seq, _ = x.shape
    m = batch * seq
    row = lambda v: v.reshape(1, -1).astype(F32)

    inv_freq = ROPE_THETA ** (-jnp.arange(0, ROPE_DIM, 2, dtype=F32) / ROPE_DIM)
    freq = jnp.tile(inv_freq, LANES // ROPE_HALF).reshape(1, LANES)
    pos2 = positions.reshape(m, 1)

    h = x.reshape(m, D_MODEL)
    for i in range(DEPTH):
        lambda_init = 0.8 - 0.6 * math.exp(-0.3 * i)
        z = _in_proj(h, pos2, freq, row(g_mix[i]), w_in[i].astype(BF16))
        attn = _diff_attn(z, row(lambda_q1[i]), row(lambda_k1[i]), row(lambda_q2[i]),
                          row(lambda_k2[i]), row(g_subln[i]),
                          batch=batch, seq=seq, lambda_init=lambda_init)
        gm = _gmlp(z, row(gmlp_ln_g[i]), row(gmlp_ln_b[i]),
                   w_spatial[i].astype(BF16), b_spatial[i].T.astype(F32))
        h1, a2 = _out_proj(attn, gm, h, w_out[i].astype(BF16), row(g_ffn[i]))
        act = _ffn_up(a2, w_up[i].astype(BF16), conv_w[i].astype(F32), row(conv_b[i]), seq=seq)
        h2, a3 = _ffn_down(act, w_down[i].astype(BF16), h1, row(g_ple[i]))
        final = i == DEPTH - 1
        h = _ple(a3, h2, p[i].reshape(m, PLE_DIM), w_ple_gate[i].astype(BF16),
                 w_ple_up[i].astype(BF16), row(g_final), final=final)
    return h.reshape(batch, seq, D_MODEL)
```

```python
import functools
import math

import jax
import jax.numpy as jnp
from jax import lax
from jax.experimental import pallas as pl
from jax.experimental.pallas import tpu as pltpu

D_MODEL = 2048
DEPTH = 1
ATTN_WIDTH = D_MODEL // 2
GMLP_WIDTH = D_MODEL - ATTN_WIDTH
ATTN_HEAD_DIM = 128
N_ATTN_HEADS = ATTN_WIDTH // ATTN_HEAD_DIM
DIFF_HALF = ATTN_HEAD_DIM // 2
ROPE_DIM = DIFF_HALF // 4
ROPE_HALF = ROPE_DIM // 2
ROPE_THETA = 500000.0
N_GMLP_HEADS = 8
GMLP_HEAD_DIM = GMLP_WIDTH // N_GMLP_HEADS
CHUNK = 128
QK_WIDTH = N_ATTN_HEADS * 2 * DIFF_HALF
IN_WIDTH = 2 * QK_WIDTH + ATTN_WIDTH + 2 * GMLP_WIDTH
D_FF = 5632
PLE_DIM = 256
EPS = 1e-6

LANES = 128
SUBLANES = 8
BF16_ROWS = 16
PERM_TILE = 512
PERM_GROUPS = PERM_TILE // SUBLANES
VMEM_LIMIT = 52 * 1024 * 1024

F32 = jnp.float32
BF16 = jnp.bfloat16


def _params(n_axes):
    return pltpu.CompilerParams(
        dimension_semantics=("arbitrary",) * n_axes, vmem_limit_bytes=VMEM_LIMIT)


def _rms(x, g):
    return x * lax.rsqrt(jnp.mean(x * x, axis=-1, keepdims=True) + EPS) * g


def _gelu(x):
    return 0.5 * x * (1.0 + lax.erf(x * (2.0 ** -0.5)))


def _sigmoid(x):
    return 1.0 / (1.0 + jnp.exp(-x))


def _in_proj_kernel(x_ref, pos_ref, freq_ref, g_ref, w_ref, z_ref,
                    a_sc, cos_sc, sin_up_sc, sin_dn_sc, *, tn):
    j = pl.program_id(1)
    n_q_tiles = QK_WIDTH // tn
    n_rope_tiles = 2 * QK_WIDTH // tn

    @pl.when(j == 0)
    def _():
        a_sc[...] = _rms(x_ref[...], g_ref[...]).astype(BF16)
        lane = lax.broadcasted_iota(jnp.int32, (1, LANES), 1)
        d = lane % DIFF_HALF
        ang = pos_ref[...].astype(F32) * freq_ref[...]
        c = jnp.cos(ang)
        s = jnp.sin(ang)
        cos_sc[...] = jnp.where(d < ROPE_DIM, c, 1.0)
        sin_up_sc[...] = jnp.where(d < ROPE_HALF, -s, 0.0)
        sin_dn_sc[...] = jnp.where((d >= ROPE_HALF) & (d < ROPE_DIM), s, 0.0)

    @pl.when(j < n_rope_tiles)
    def _():
        acc = jnp.dot(a_sc[...], w_ref[...], preferred_element_type=F32)
        scale = jnp.where(j < n_q_tiles, DIFF_HALF ** -0.5 * math.log2(math.e), 1.0).astype(F32)
        for c0 in range(0, tn, LANES):
            t = acc[:, c0:c0 + LANES]
            r = (t * cos_sc[...]
                 + pltpu.roll(t, LANES - ROPE_HALF, 1) * sin_up_sc[...]
                 + pltpu.roll(t, ROPE_HALF, 1) * sin_dn_sc[...])
            z_ref[:, c0:c0 + LANES] = (r * scale).astype(BF16)

    @pl.when(j >= n_rope_tiles)
    def _():
        z_ref[...] = jnp.dot(a_sc[...], w_ref[...], preferred_element_type=F32).astype(BF16)


def _in_proj(x2, pos2, freq, g, w, *, tm=1024, tn=512):
    m = x2.shape[0]
    return pl.pallas_call(
        functools.partial(_in_proj_kernel, tn=tn),
        grid=(m // tm, IN_WIDTH // tn),
        in_specs=[
            pl.BlockSpec((tm, D_MODEL), lambda i, j: (i, 0)),
            pl.BlockSpec((tm, 1), lambda i, j: (i, 0)),
            pl.BlockSpec((1, LANES), lambda i, j: (0, 0)),
            pl.BlockSpec((1, D_MODEL), lambda i, j: (0, 0)),
            pl.BlockSpec((D_MODEL, tn), lambda i, j: (0, j)),
        ],
        out_specs=pl.BlockSpec((tm, tn), lambda i, j: (i, j)),
        out_shape=jax.ShapeDtypeStruct((m, IN_WIDTH), BF16),
        scratch_shapes=[
            pltpu.VMEM((tm, D_MODEL), BF16),
            pltpu.VMEM((tm, LANES), F32),
            pltpu.VMEM((tm, LANES), F32),
            pltpu.VMEM((tm, LANES), F32),
        ],
        compiler_params=_params(2),
        name="in_proj",
    )(x2, pos2, freq, g, w)


def _diff_attn_kernel(q_ref, k_ref, v_ref, lq1_ref, lk1_ref, lq2_ref, lk2_ref, g_ref,
                      o_ref, vext_sc, *, lambda_init, sub):
    seq = q_ref.shape[0]
    vext_sc[:, :ATTN_HEAD_DIM] = v_ref[...]
    vext_sc[:, ATTN_HEAD_DIM:] = jnp.ones((seq, ATTN_HEAD_DIM), BF16)

    lam = (jnp.exp(jnp.sum(lq1_ref[...] * lk1_ref[...], axis=-1, keepdims=True))
           - jnp.exp(jnp.sum(lq2_ref[...] * lk2_ref[...], axis=-1, keepdims=True))
           + lambda_init)

    k = k_ref[...]
    lane = lax.broadcasted_iota(jnp.int32, (1, LANES), 1)
    nt = (((1,), (1,)), ((), ()))

    def component(qc):
        s = lax.dot_general(qc, k, nt, preferred_element_type=F32)
        e = jnp.exp2(s - jnp.max(s, axis=-1, keepdims=True)).astype(BF16)
        oe = jnp.dot(e, vext_sc[...], preferred_element_type=F32)
        return oe[:, :ATTN_HEAD_DIM] / oe[:, ATTN_HEAD_DIM:ATTN_HEAD_DIM + 1]

    for r0 in range(0, seq, sub):
        q = q_ref[r0:r0 + sub, :]
        o1 = component(jnp.where(lane < DIFF_HALF, q, jnp.zeros_like(q)))
        o2 = component(jnp.where(lane >= DIFF_HALF, q, jnp.zeros_like(q)))
        o = o1 - lam * o2
        o_ref[r0:r0 + sub, :] = (_rms(o, g_ref[...]) * (1.0 - lambda_init)).astype(BF16)


def _diff_attn(z, lq1, lk1, lq2, lk2, g, *, batch, seq, lambda_init, sub=256):
    head_blk = lambda col0: pl.BlockSpec((seq, ATTN_HEAD_DIM), lambda b, h: (b, col0 + h))
    vec_spec = lambda n: pl.BlockSpec((1, n), lambda b, h: (0, 0))
    return pl.pallas_call(
        functools.partial(_diff_attn_kernel, lambda_init=lambda_init, sub=sub),
        grid=(batch, N_ATTN_HEADS),
        in_specs=[
            head_blk(0),
            head_blk(QK_WIDTH // ATTN_HEAD_DIM),
            head_blk(2 * QK_WIDTH // ATTN_HEAD_DIM),
            vec_spec(DIFF_HALF), vec_spec(DIFF_HALF), vec_spec(DIFF_HALF), vec_spec(DIFF_HALF),
            vec_spec(ATTN_HEAD_DIM),
        ],
        out_specs=head_blk(0),
        out_shape=jax.ShapeDtypeStruct((batch * seq, ATTN_WIDTH), BF16),
        scratch_shapes=[pltpu.VMEM((seq, 2 * ATTN_HEAD_DIM), BF16)],
        compiler_params=_params(2),
        name="diff_attn",
    )(z, z, z, lq1, lk1, lq2, lk2, g)


def _gmlp_kernel(u_ref, v_ref, lng_ref, lnb_ref, ws_ref, bs_ref, o_ref, *, rows):
    v = _gelu(v_ref[...].astype(F32))
    mu = jnp.mean(v, axis=-1, keepdims=True)
    vc = v - mu
    var = jnp.mean(vc * vc, axis=-1, keepdims=True)
    vn = (vc * lax.rsqrt(var + EPS) * lng_ref[...] + lnb_ref[...]).astype(BF16)
    for c0 in range(0, rows, CHUNK):
        for h in range(N_GMLP_HEADS):
            cols = slice(h * GMLP_HEAD_DIM, (h + 1) * GMLP_HEAD_DIM)
            mixed = jnp.dot(ws_ref[h], vn[c0:c0 + CHUNK, cols], preferred_element_type=F32)
            mixed = mixed + bs_ref[:, h:h + 1]
            u = _gelu(u_ref[c0:c0 + CHUNK, cols].astype(F32))
            o_ref[c0:c0 + CHUNK, cols] = (u * mixed).astype(BF16)


def _gmlp(z, ln_g, ln_b, ws, bs_t, *, rows=512):
    m = z.shape[0]
    u_blk = (2 * QK_WIDTH + ATTN_WIDTH) // GMLP_WIDTH
    const2 = lambda i: (0, 0)
    return pl.pallas_call(
        functools.partial(_gmlp_kernel, rows=rows),
        grid=(m // rows,),
        in_specs=[
            pl.BlockSpec((rows, GMLP_WIDTH), lambda i: (i, u_blk)),
            pl.BlockSpec((rows, GMLP_WIDTH), lambda i: (i, u_blk + 1)),
            pl.BlockSpec((1, GMLP_WIDTH), const2),
            pl.BlockSpec((1, GMLP_WIDTH), const2),
            pl.BlockSpec((N_GMLP_HEADS, CHUNK, CHUNK), lambda i: (0, 0, 0)),
            pl.BlockSpec((CHUNK, N_GMLP_HEADS), const2),
        ],
        out_specs=pl.BlockSpec((rows, GMLP_WIDTH), lambda i: (i, 0)),
        out_shape=jax.ShapeDtypeStruct((m, GMLP_WIDTH), BF16),
        compiler_params=_params(1),
        name="gmlp",
    )(z, z, ln_g, ln_b, ws, bs_t)


def _out_proj_kernel(attn_ref, gm_ref, x_ref, w_ref, g_ref, h_ref, a_ref):
    acc = jnp.dot(attn_ref[...], w_ref[:ATTN_WIDTH, :], preferred_element_type=F32)
    acc = acc + jnp.dot(gm_ref[...], w_ref[ATTN_WIDTH:, :], preferred_element_type=F32)
    h = x_ref[...] + acc
    a = _rms(h, g_ref[...]).astype(BF16)
    for q in range(h.shape[0] // PERM_GROUPS):
        rows = slice(q * PERM_GROUPS, (q + 1) * PERM_GROUPS)
        cols = slice(q * D_MODEL, (q + 1) * D_MODEL)
        h_ref[0, :, cols] = h[rows]
        a_ref[0, :, cols] = a[rows]


def _perm_spec(tm):
    per_tile = PERM_TILE // tm
    return pl.BlockSpec((1, PERM_GROUPS, (tm // PERM_GROUPS) * D_MODEL),
                        lambda i: (i // per_tile, 0, i % per_tile))


def _out_proj(attn, gm, x2, w, g, *, tm=256):
    m = x2.shape[0]
    row = lambda n: pl.BlockSpec((tm, n), lambda i: (i, 0))
    perm_shape = (m // PERM_TILE, PERM_GROUPS, SUBLANES * D_MODEL)
    return pl.pallas_call(
        _out_proj_kernel,
        grid=(m // tm,),
        in_specs=[
            row(ATTN_WIDTH), row(GMLP_WIDTH), row(D_MODEL),
            pl.BlockSpec((D_MODEL, D_MODEL), lambda i: (0, 0), pipeline_mode=pl.Buffered(1)),
            pl.BlockSpec((1, D_MODEL), lambda i: (0, 0)),
        ],
        out_specs=[_perm_spec(tm), _perm_spec(tm)],
        out_shape=[jax.ShapeDtypeStruct(perm_shape, F32),
                   jax.ShapeDtypeStruct(perm_shape, BF16)],
        compiler_params=_params(1),
        name="out_proj",
    )(attn, gm, x2, w, g)


def _ffn_up_kernel(a_ref, prev_ref, next_ref, wg_ref, wu_ref, cwg_ref, cwu_ref,
                   cbg_ref, cbu_ref, o_ref, ext_sc, wg_sc, wu_sc, hg_sc, hu_sc,
                   *, tm, tn, seq, chunk):
    i = pl.program_id(1)
    halo = BF16_ROWS
    sub = SUBLANES

    @pl.when(i == 0)
    def _():
        wg_sc[...] = wg_ref[...].astype(BF16)
        wu_sc[...] = wu_ref[...].astype(BF16)

    first = (i * tm) % seq == 0
    last = ((i + 1) * tm) % seq == 0
    ext_sc[halo:halo + tm, :] = a_ref[...]
    ext_sc[:halo, :] = jnp.where(first, jnp.zeros_like(prev_ref[...]), prev_ref[...])
    ext_sc[halo + tm:, :] = jnp.where(last, jnp.zeros_like(next_ref[...]), next_ref[...])

    def conv(h_sc, w_sc, cw_ref, cb_ref, cols):
        h_sc[:, cols] = jnp.dot(ext_sc[...], w_sc[:, cols], preferred_element_type=F32)
        lo = jnp.concatenate([h_sc[halo - 1:halo, cols],
                              h_sc[halo + tm - sub:halo + tm - 1, cols]], axis=0)
        hi = jnp.concatenate([h_sc[halo + 1:halo + sub, cols],
                              h_sc[halo + tm:halo + tm + 1, cols]], axis=0)
        h_sc[halo - sub:halo, cols] = lo
        h_sc[halo + tm:halo + tm + sub, cols] = hi
        return (h_sc[halo - sub:halo - sub + tm, cols] * cw_ref[0:1, cols]
                + h_sc[halo:halo + tm, cols] * cw_ref[1:2, cols]
                + h_sc[halo + sub:halo + sub + tm, cols] * cw_ref[2:3, cols]
                + cb_ref[:, cols])

    for c0 in range(0, tn, chunk):
        cols = slice(c0, c0 + chunk)
        g = conv(hg_sc, wg_sc, cwg_ref, cbg_ref, cols)
        u = conv(hu_sc, wu_sc, cwu_ref, cbu_ref, cols)
        o_ref[:, cols] = (g * _sigmoid(g) * u).astype(BF16)


def _ffn_up(a2, w_up, conv_w, conv_b, *, seq, tn=512, chunk=256):
    m = a2.shape[0]
    tm = PERM_TILE
    nj = D_FF // tn
    halo = BF16_ROWS
    hb = tm // halo
    n_hb = m // halo
    return pl.pallas_call(
        functools.partial(_ffn_up_kernel, tm=tm, tn=tn, seq=seq, chunk=chunk),
        grid=(nj, m // tm),
        in_specs=[
            pl.BlockSpec((tm, D_MODEL), lambda j, i: (i, 0)),
            pl.BlockSpec((halo, D_MODEL), lambda j, i: (jnp.maximum(i * hb - 1, 0), 0)),
            pl.BlockSpec((halo, D_MODEL), lambda j, i: (jnp.minimum((i + 1) * hb, n_hb - 1), 0)),
            pl.BlockSpec((D_MODEL, tn), lambda j, i: (0, j)),
            pl.BlockSpec((D_MODEL, tn), lambda j, i: (0, nj + j)),
            pl.BlockSpec((3, tn), lambda j, i: (0, j)),
            pl.BlockSpec((3, tn), lambda j, i: (0, nj + j)),
            pl.BlockSpec((1, tn), lambda j, i: (0, j)),
            pl.BlockSpec((1, tn), lambda j, i: (0, nj + j)),
        ],
        out_specs=pl.BlockSpec((tm, tn), lambda j, i: (i, j)),
        out_shape=jax.ShapeDtypeStruct((m, D_FF), BF16),
        scratch_shapes=[
            pltpu.VMEM((tm + 2 * halo, D_MODEL), BF16),
            pltpu.VMEM((D_MODEL, tn), BF16),
            pltpu.VMEM((D_MODEL, tn), BF16),
            pltpu.VMEM((tm + 2 * halo, tn), F32),
            pltpu.VMEM((tm + 2 * halo, tn), F32),
        ],
        compiler_params=_params(2),
        name="ffn_up",
    )(a2, a2, a2, w_up, w_up, conv_w, conv_w, conv_b, conv_b)


def _ffn_down_kernel(act_ref, w_ref, h1_ref, g_ref, h2_ref, a3_ref):
    h2 = h1_ref[...] + jnp.dot(act_ref[...], w_ref[...], preferred_element_type=F32)
    h2_ref[...] = h2
    a3_ref[...] = _rms(h2, g_ref[...]).astype(BF16)


def _ffn_down(act, w, h1, g, *, tm=256):
    m = act.shape[0]
    row = pl.BlockSpec((tm, D_MODEL), lambda i: (i, 0))
    return pl.pallas_call(
        _ffn_down_kernel,
        grid=(m // tm,),
        in_specs=[
            pl.BlockSpec((tm, D_FF), lambda i: (i, 0)),
            pl.BlockSpec((D_FF, D_MODEL), lambda i: (0, 0), pipeline_mode=pl.Buffered(1)),
            row,
            pl.BlockSpec((1, D_MODEL), lambda i: (0, 0)),
        ],
        out_specs=[row, row],
        out_shape=[jax.ShapeDtypeStruct((m, D_MODEL), F32),
                   jax.ShapeDtypeStruct((m, D_MODEL), BF16)],
        compiler_params=_params(1),
        name="ffn_down",
    )(act, w, h1, g)


def _ple_kernel(a3_ref, h2_ref, p_ref, wg_ref, wu_ref, gn_ref, o_ref, *, final):
    def natural(ref):
        n = ref.shape[2] // D_MODEL
        return jnp.concatenate(
            [ref[0, :, q * D_MODEL:(q + 1) * D_MODEL] for q in range(n)], axis=0)

    gate = _sigmoid(jnp.dot(natural(a3_ref), wg_ref[...], preferred_element_type=F32))
    pe = jnp.dot(p_ref[...].astype(BF16), wu_ref[...], preferred_element_type=F32)
    h3 = natural(h2_ref) + pe * gate
    o_ref[...] = _rms(h3, gn_ref[...]) if final else h3


def _ple(a3, h2, p2, w_gate, w_up, g_norm, *, final, tm=256):
    m = p2.shape[0]
    row = lambda n: pl.BlockSpec((tm, n), lambda i: (i, 0))
    return pl.pallas_call(
        functools.partial(_ple_kernel, final=final),
        grid=(m // tm,),
        in_specs=[
            _perm_spec(tm), _perm_spec(tm), row(PLE_DIM),
            pl.BlockSpec((D_MODEL, D_MODEL), lambda i: (0, 0), pipeline_mode=pl.Buffered(1)),
            pl.BlockSpec((PLE_DIM, D_MODEL), lambda i: (0, 0)),
            pl.BlockSpec((1, D_MODEL), lambda i: (0, 0)),
        ],
        out_specs=row(D_MODEL),
        out_shape=jax.ShapeDtypeStruct((m, D_MODEL), F32),
        compiler_params=_params(1),
        name="ple",
    )(a3, h2, p2, w_gate, w_up, g_norm)


def kernel(x, p, positions, g_mix, w_in, lambda_q1, lambda_k1, lambda_q2, lambda_k2, g_subln, gmlp_ln_g, gmlp_ln_b, w_spatial, b_spatial, w_out, g_ffn, w_up, conv_w, conv_b, w_down, g_ple, w_ple_gate, w_ple_up, g_final):
    batch, seq, _ = x.shape
    m = batch * seq
    row = lambda v: v.reshape(1, -1).astype(F32)

    inv_freq = ROPE_THETA ** (-jnp.arange(0, ROPE_DIM, 2, dtype=F32) / ROPE_DIM)
    freq = jnp.tile(inv_freq, LANES // ROPE_HALF).reshape(1, LANES)
    pos2 = positions.reshape(m, 1)

    h = x.reshape(m, D_MODEL)
    for i in range(DEPTH):
        lambda_init = 0.8 - 0.6 * math.exp(-0.3 * i)
        z = _in_proj(h, pos2, freq, row(g_mix[i]), w_in[i].astype(BF16))
        attn = _diff_attn(z, row(lambda_q1[i]), row(lambda_k1[i]), row(lambda_q2[i]),
                          row(lambda_k2[i]), row(g_subln[i]),
                          batch=batch, seq=seq, lambda_init=lambda_init)
        gm = _gmlp(z, row(gmlp_ln_g[i]), row(gmlp_ln_b[i]),
                   w_spatial[i].astype(BF16), b_spatial[i].T.astype(F32))
        h1, a2 = _out_proj(attn, gm, h, w_out[i].astype(BF16), row(g_ffn[i]))
        perm_shape = h1.shape
        act = _ffn_up(a2.reshape(m, D_MODEL), w_up[i], conv_w[i].astype(F32), row(conv_b[i]),
                      seq=seq)
        h2, a3 = _ffn_down(act, w_down[i].astype(BF16), h1.reshape(m, D_MODEL), row(g_ple[i]))
        final = i == DEPTH - 1
        h = _ple(a3.reshape(perm_shape), h2.reshape(perm_shape), p[i].reshape(m, PLE_DIM),
                 w_ple_gate[i].astype(BF16), w_ple_up[i].astype(BF16), row(g_final),
                 final=final)
    return h.reshape(batch, seq, D_MODEL)
```
